```python
import math
import jax, jax.numpy as jnp
from jax import lax
import numpy as np

D_MODEL = 1024
BATCH = 16
SEQ = 4096
DEPTH = 1

HEAD_DIM = 64
N_Q_HEADS = 8
N_KV_HEADS = 2
Q_GROUP = N_Q_HEADS // N_KV_HEADS
ATTN_WIDTH = N_Q_HEADS * HEAD_DIM
KV_WIDTH = N_KV_HEADS * HEAD_DIM
WINDOW = 128
ATTN_BLOCK = 128
ROPE_DIM = HEAD_DIM // 4
ROPE_THETA = 500000.0
N_G_HEADS = 8
G_HEAD_DIM = 64
GMLP_WIDTH = N_G_HEADS * G_HEAD_DIM
CHUNK = 128
MIX_WIDTH = ATTN_WIDTH + GMLP_WIDTH
IN_WIDTH = ATTN_WIDTH + 2 * KV_WIDTH + 2 * GMLP_WIDTH
SPLITS = (ATTN_WIDTH, ATTN_WIDTH + KV_WIDTH, ATTN_WIDTH + 2 * KV_WIDTH,
          ATTN_WIDTH + 2 * KV_WIDTH + GMLP_WIDTH)
N_EXPERTS = 32
TOP_K = 4
D_EXPERT = D_MODEL
SWIGLU_LIMIT = 7.0
SWIGLU_ALPHA = 1.702
EXPERT_BLOCK = 256
LN_EPS = 1e-5
DEEPNORM_ALPHA = (2.0 * DEPTH) ** 0.25
DEEPNORM_BETA = (8.0 * DEPTH) ** -0.25

kernel_name = "hybrid_swa_sink_gmlp_moe_deepnorm"


def layer_norm(x, g, b):
    xf = x.astype(jnp.float32)
    mu = jnp.mean(xf, axis=-1, keepdims=True)
    var = jnp.mean(jnp.square(xf - mu), axis=-1, keepdims=True)
    y = (xf - mu) * lax.rsqrt(var + LN_EPS)
    return (y * g.astype(jnp.float32) + b.astype(jnp.float32)).astype(x.dtype)


def rotary_tables(positions):
    inv_freq = ROPE_THETA ** (-jnp.arange(0, ROPE_DIM, 2, dtype=jnp.float32) / ROPE_DIM)
    ang = positions.astype(jnp.float32)[..., None] * inv_freq
    return jnp.cos(ang)[:, :, None, :], jnp.sin(ang)[:, :, None, :]


def apply_partial_rope(t, cos, sin):
    half = ROPE_DIM // 2
    cos = cos.astype(t.dtype)
    sin = sin.astype(t.dtype)
    r1, r2, rest = t[..., :half], t[..., half:ROPE_DIM], t[..., ROPE_DIM:]
    return jnp.concatenate([r1 * cos - r2 * sin, r2 * cos + r1 * sin, rest], axis=-1)


def sliding_window_attention(q, k, v, sinks):
    B, S = q.shape[0], q.shape[1]
    nb = S // ATTN_BLOCK
    qb = q.reshape(B, nb, ATTN_BLOCK, N_KV_HEADS, Q_GROUP, HEAD_DIM)
    kb = k.reshape(B, nb, ATTN_BLOCK, N_KV_HEADS, HEAD_DIM)
    vb = v.reshape(B, nb, ATTN_BLOCK, N_KV_HEADS, HEAD_DIM)
    pad = ((0, 0), (1, 0), (0, 0), (0, 0), (0, 0))
    kw = jnp.concatenate([jnp.pad(kb[:, :-1], pad), kb], axis=2)
    vw = jnp.concatenate([jnp.pad(vb[:, :-1], pad), vb], axis=2)
    s = jnp.einsum('bnqhgd,bnkhd->bnhgqk', qb, kw).astype(jnp.float32) * (HEAD_DIM ** -0.5)
    qi = jnp.arange(ATTN_BLOCK)[:, None] + ATTN_BLOCK
    kj = jnp.arange(2 * ATTN_BLOCK)[None, :]
    diff = qi - kj
    band = (diff >= 0) & (diff < WINDOW)
    has_prev = (jnp.arange(nb)[:, None, None] > 0) | (kj[None] >= ATTN_BLOCK)
    mask = band[None] & has_prev
    s = jnp.where(mask[None, :, None, None], s, -jnp.inf)
    sink = sinks.astype(jnp.float32).reshape(N_KV_HEADS, Q_GROUP)[None, None, :, :, None, None]
    m = jnp.maximum(jnp.max(s, axis=-1, keepdims=True), sink)
    p = jnp.exp(s - m)
    p = p / (jnp.sum(p, axis=-1, keepdims=True) + jnp.exp(sink - m))
    o = jnp.einsum('bnhgqk,bnkhd->bnqhgd', p.astype(v.dtype), vw)
    return o.reshape(B, S, ATTN_WIDTH)


def chunked_spatial_gating(u, v, gn_g, gn_b, w_s, b_s):
    B, S = u.shape[0], u.shape[1]
    nc = S // CHUNK
    v = layer_norm(v, gn_g, gn_b)
    vc = v.reshape(B, nc, CHUNK, N_G_HEADS, G_HEAD_DIM)
    tril = jnp.tril(jnp.ones((CHUNK, CHUNK), dtype=bool))
    w = jnp.where(tril[None], w_s, jnp.zeros((), w_s.dtype))
    mixed = jnp.einsum('hts,bcshd->bcthd', w, vc) + b_s.T[None, None, :, :, None]
    return (u.reshape(B, nc, CHUNK, N_G_HEADS, G_HEAD_DIM) * mixed).reshape(B, S, GMLP_WIDTH)


def hybrid_mixer(x, cos, sin, w_in, b_in, sinks, gn_g, gn_b, w_s, b_s, w_out, b_out):
    B, S, _ = x.shape
    proj = jnp.einsum('bsd,de->bse', x, w_in) + b_in
    q, k, v, gu, gv = jnp.split(proj, SPLITS, axis=-1)
    q = apply_partial_rope(q.reshape(B, S, N_Q_HEADS, HEAD_DIM), cos, sin)
    k = apply_partial_rope(k.reshape(B, S, N_KV_HEADS, HEAD_DIM), cos, sin)
    v = v.reshape(B, S, N_KV_HEADS, HEAD_DIM)
    attn = sliding_window_attention(q, k, v, sinks)
    gu = jax.nn.gelu(gu, approximate=False).reshape(B, S, N_G_HEADS, G_HEAD_DIM)
    gv = jax.nn.gelu(gv, approximate=False).reshape(B, S, N_G_HEADS, G_HEAD_DIM)
    gmix = chunked_spatial_gating(gu, gv, gn_g, gn_b, w_s, b_s)
    o = jnp.concatenate([attn, gmix], axis=-1)
    return jnp.einsum('bse,ed->bsd', o, w_out) + b_out


def clamped_swiglu(gu):
    gate, up = gu[..., :D_EXPERT], gu[..., D_EXPERT:]
    gate = jnp.minimum(gate, SWIGLU_LIMIT)
    up = jnp.clip(up, -SWIGLU_LIMIT, SWIGLU_LIMIT)
    return (up + 1.0) * (gate * jax.nn.sigmoid(gate * SWIGLU_ALPHA))


def routed_experts(x, w_router, b_router, w_gate_up, b_gate_up, w_down, b_down):
    B, S, D = x.shape
    T = B * S
    x2 = x.reshape(T, D)
    logits = (x2 @ w_router + b_router).astype(jnp.float32)
    top_val, top_idx = lax.top_k(logits, TOP_K)
    gates = jax.nn.softmax(top_val, axis=-1).astype(x.dtype)
    N = T * TOP_K
    flat_e = top_idx.reshape(N)
    flat_tok = jnp.arange(N, dtype=jnp.int32) // TOP_K
    flat_g = gates.reshape(N)
    order = jnp.argsort(flat_e)
    sorted_e = flat_e[order]
    counts = jnp.bincount(flat_e, length=N_EXPERTS)
    starts = jnp.cumsum(counts) - counts
    padded = ((counts + EXPERT_BLOCK - 1) // EXPERT_BLOCK) * EXPERT_BLOCK
    pends = jnp.cumsum(padded)
    pstarts = pends - padded
    dest = pstarts[sorted_e] + (jnp.arange(N) - starts[sorted_e])
    NP = N + N_EXPERTS * EXPERT_BLOCK
    nblk = NP // EXPERT_BLOCK
    buf_tok = jnp.zeros((NP,), jnp.int32).at[dest].set(flat_tok[order])
    buf_g = jnp.zeros((NP,), x.dtype).at[dest].set(flat_g[order])
    blk_start = jnp.arange(nblk) * EXPERT_BLOCK
    blk_e = jnp.minimum(jnp.searchsorted(pends, blk_start, side='right'), N_EXPERTS - 1)

    def body(acc, blk):
        tok, g, e = blk
        xb = x2[tok]
        h = clamped_swiglu(xb @ w_gate_up[e] + b_gate_up[e])
        y = (h @ w_down[e] + b_down[e]) * g[:, None]
        return acc.at[tok].add(y.astype(acc.dtype)), None

    acc0 = jnp.zeros((T, D), x.dtype)
    out, _ = lax.scan(body, acc0, (buf_tok.reshape(nblk, EXPERT_BLOCK),
                                   buf_g.reshape(nblk, EXPERT_BLOCK), blk_e))
    return out.reshape(B, S, D)


def setup_inputs(seed: int = 0) -> dict:
    key = jax.random.key(seed)
    ks = jax.random.split(key, 24)
    f32 = jnp.float32
    L = DEPTH
    nrm = lambda k, shape, scale: jax.random.normal(k, shape, f32) * scale
    x = jax.random.normal(ks[0], (BATCH, SEQ, D_MODEL), f32)
    offsets = jax.random.randint(ks[1], (BATCH, 1), 0, 4096, dtype=jnp.int32)
    positions = offsets + jnp.arange(SEQ, dtype=jnp.int32)[None, :]
    col_scale = jnp.concatenate([jnp.ones((ATTN_WIDTH + KV_WIDTH,), f32),
                                 jnp.full((KV_WIDTH + 2 * GMLP_WIDTH,), DEEPNORM_BETA, f32)])
    return {
        "x": x,
        "positions": positions,
        "ln_in_g": 1.0 + nrm(ks[2], (D_MODEL,), 0.02),
        "ln_in_b": nrm(ks[3], (D_MODEL,), 0.02),
        "w_in": nrm(ks[4], (L, D_MODEL, IN_WIDTH), D_MODEL ** -0.5) * col_scale,
        "b_in": nrm(ks[5], (L, IN_WIDTH), 0.02),
        "sinks": nrm(ks[6], (L, N_Q_HEADS), 0.5),
        "gn_g": 1.0 + nrm(ks[7], (L, G_HEAD_DIM), 0.02),
        "gn_b": nrm(ks[8], (L, G_HEAD_DIM), 0.02),
        "w_s": nrm(ks[9], (L, N_G_HEADS, CHUNK, CHUNK), CHUNK ** -0.5),
        "b_s": 1.0 + nrm(ks[10], (L, N_G_HEADS, CHUNK), 0.02),
        "w_out": nrm(ks[11], (L, MIX_WIDTH, D_MODEL), MIX_WIDTH ** -0.5 * DEEPNORM_BETA),
        "b_out": nrm(ks[12], (L, D_MODEL), 0.02),
        "ln1_g": 1.0 + nrm(ks[13], (L, D_MODEL), 0.02),
        "ln1_b": nrm(ks[14], (L, D_MODEL), 0.02),
        "w_router": nrm(ks[15], (L, D_MODEL, N_EXPERTS), D_MODEL ** -0.5),
        "b_router": nrm(ks[16], (L, N_EXPERTS), 0.01),
        "w_gate_up": nrm(ks[17], (L, N_EXPERTS, D_MODEL, 2 * D_EXPERT), D_MODEL ** -0.5 * DEEPNORM_BETA),
        "b_gate_up": nrm(ks[18], (L, N_EXPERTS, 2 * D_EXPERT), 0.02),
        "w_down": nrm(ks[19], (L, N_EXPERTS, D_EXPERT, D_MODEL), D_EXPERT ** -0.5 * DEEPNORM_BETA),
        "b_down": nrm(ks[20], (L, N_EXPERTS, D_MODEL), 0.02),
        "ln2_g": 1.0 + nrm(ks[21], (L, D_MODEL), 0.02),
        "ln2_b": nrm(ks[22], (L, D_MODEL), 0.02),
    }


def reference(x, positions, ln_in_g, ln_in_b, w_in, b_in, sinks, gn_g, gn_b, w_s, b_s,
              w_out, b_out, ln1_g, ln1_b, w_router, b_router, w_gate_up, b_gate_up,
              w_down, b_down, ln2_g, ln2_b):
    cos, sin = rotary_tables(positions)
    h = layer_norm(x, ln_in_g, ln_in_b)
    for l in range(DEPTH):
        mix = hybrid_mixer(h, cos, sin, w_in[l], b_in[l], sinks[l], gn_g[l], gn_b[l],
                           w_s[l], b_s[l], w_out[l], b_out[l])
        h = layer_norm(DEEPNORM_ALPHA * h + mix, ln1_g[l], ln1_b[l])
        ffn = routed_experts(h, w_router[l], b_router[l], w_gate_up[l], b_gate_up[l],
                             w_down[l], b_down[l])
        h = layer_norm(DEEPNORM_ALPHA * h + ffn, ln2_g[l], ln2_b[l])
    return h
```

```python
import functools
import math

import numpy as np
import jax
import jax.numpy as jnp
from jax import lax
from jax.experimental import pallas as pl
from jax.experimental.pallas import tpu as pltpu

F32 = jnp.float32
BF16 = jnp.bfloat16

D_MODEL = 1024
SEQ = 4096
HEAD_DIM = 64
N_Q_HEADS = 8
N_KV_HEADS = 2
ATTN_WIDTH = 512
KV_WIDTH = 128
ATTN_BLOCK = 128
ROPE_DIM = 16
ROPE_THETA = 500000.0
N_G_HEADS = 8
GMLP_WIDTH = 512
IN_WIDTH = 1792
N_EXPERTS = 32
TOP_K = 4
D_EXPERT = 1024
SWIGLU_LIMIT = 7.0
SWIGLU_ALPHA = 1.702
LN_EPS = 1e-5
DEEPNORM_ALPHA = 2.0 ** 0.25

LANES = 128
VMEM_LIMIT_BYTES = 56 * 1024 * 1024

MIX_ROWS = 256
EXPERT_ROWS = 256
EXPERT_CHUNK = 512
FIN_ROWS = 256

_NEG_INF = float("-inf")


def _layer_norm(x, g, b):
    mu = jnp.mean(x, axis=-1, keepdims=True)
    xc = x - mu
    var = jnp.mean(xc * xc, axis=-1, keepdims=True)
    return xc * lax.rsqrt(var + LN_EPS) * g + b


def _gelu(x):
    return 0.5 * x * (1.0 + lax.erf(x * (1.0 / math.sqrt(2.0))))


def _split_bf16(t):
    hi = t.astype(BF16)
    lo = (t - hi.astype(F32)).astype(BF16)
    return hi, lo


def _dot(a, b):
    return jnp.dot(a, b, preferred_element_type=F32)


def _mix_kernel(sink_ref, x_ref, pos_ref, lng_ref, lnb_ref, win_ref, bin_ref, frow_ref,
                gng_ref, gnb_ref, ws_ref, bsf_ref, amat_ref, wout_ref, bout_ref,
                ln1g_ref, ln1b_ref, wrh_ref, wrl_ref, br_ref,
                h1_ref, idx_ref, gate_ref,
                wsp_s, ksel_s, vsel_s, ocat_s):
    pid = pl.program_id(0)
    blk = ATTN_BLOCK
    nblk = MIX_ROWS // blk

    @pl.when(pid == 0)
    def _init():
        ksel_s[...] = jnp.zeros_like(ksel_s)
        vsel_s[...] = jnp.zeros_like(vsel_s)
        row = lax.broadcasted_iota(jnp.int32, (blk, blk), 0)
        col = lax.broadcasted_iota(jnp.int32, (blk, blk), 1)
        tril = col <= row
        for j in range(N_G_HEADS // 2):
            a = jnp.where(tril, ws_ref[2 * j], 0.0)
            b = jnp.where(tril, ws_ref[2 * j + 1], 0.0)
            wsp_s[j] = jnp.concatenate([a, b], axis=1).astype(BF16)

    x = x_ref[...]
    h0 = _layer_norm(x, lng_ref[...], lnb_ref[...])
    proj = _dot(h0.astype(BF16), win_ref[...]) + bin_ref[...]

    lane = lax.broadcasted_iota(jnp.int32, (1, LANES), 1)
    lo_half = lane < HEAD_DIM
    lane_in_head = lane & (HEAD_DIM - 1)

    ang = pos_ref[...].astype(F32) * frow_ref[...]
    cos_t = jnp.cos(ang)
    sin_t = jnp.sin(ang)
    half = ROPE_DIM // 2
    s_lo = jnp.where(lane_in_head < half, -sin_t, 0.0)
    s_hi = jnp.where((lane_in_head >= half) & (lane_in_head < ROPE_DIM), sin_t, 0.0)

    def rope(t):
        return (t * cos_t + pltpu.roll(t, LANES - half, 1) * s_lo
                + pltpu.roll(t, half, 1) * s_hi)

    scale = HEAD_DIM ** -0.5
    q_cols = [(rope(proj[:, j * LANES:(j + 1) * LANES]) * scale).astype(BF16)
              for j in range(ATTN_WIDTH // LANES)]
    kr = rope(proj[:, ATTN_WIDTH:ATTN_WIDTH + KV_WIDTH])
    vv = proj[:, ATTN_WIDTH + KV_WIDTH:ATTN_WIDTH + 2 * KV_WIDTH]
    kroll = pltpu.roll(kr, HEAD_DIM, 1)
    vroll = pltpu.roll(vv, HEAD_DIM, 1)
    k_var = [jnp.where(lo_half, kr, 0.0), jnp.where(lo_half, 0.0, kroll),
             jnp.where(lo_half, kroll, 0.0), jnp.where(lo_half, 0.0, kr)]
    v_var = [jnp.where(lo_half, vv, 0.0), jnp.where(lo_half, 0.0, vroll),
             jnp.where(lo_half, vroll, 0.0), jnp.where(lo_half, 0.0, vv)]
    k_var = [t.astype(BF16) for t in k_var]
    v_var = [t.astype(BF16) for t in v_var]

    qi = lax.broadcasted_iota(jnp.int32, (blk, 2 * blk), 0)
    kj = lax.broadcasted_iota(jnp.int32, (blk, 2 * blk), 1)
    m_cur = (kj >= blk) & (kj - blk <= qi)
    m_prev = (kj < blk) & (kj > qi)

    for bi in range(nblk):
        r0 = bi * blk
        for var in range(4):
            ksel_s[var, blk:2 * blk, :] = k_var[var][r0:r0 + blk]
            vsel_s[var, blk:2 * blk, :] = v_var[var][r0:r0 + blk]
        has_prev = lax.rem(pid * MIX_ROWS + r0, SEQ) != 0
        mask = m_cur | (m_prev & has_prev)
        for j in range(ATTN_WIDTH // LANES):
            qj = q_cols[j][r0:r0 + blk]
            acc = None
            for hf in range(2):
                h = 2 * j + hf
                var = (h // (N_Q_HEADS // N_KV_HEADS)) * 2 + hf
                s = lax.dot_general(qj, ksel_s[var], (((1,), (1,)), ((), ())),
                                    preferred_element_type=F32)
                s = jnp.where(mask, s, _NEG_INF)
                sk = sink_ref[h]
                m = jnp.maximum(jnp.max(s, axis=-1, keepdims=True), sk)
                p = jnp.exp(s - m)
                den = jnp.sum(p, axis=-1, keepdims=True) + jnp.exp(sk - m)
                o = _dot(p.astype(BF16), vsel_s[var]) * (1.0 / den)
                acc = o if acc is None else acc + o
            ocat_s[r0:r0 + blk, j * LANES:(j + 1) * LANES] = acc.astype(BF16)
        for var in range(4):
            ksel_s[var, 0:blk, :] = ksel_s[var, blk:2 * blk, :]
            vsel_s[var, 0:blk, :] = vsel_s[var, blk:2 * blk, :]

    g0 = ATTN_WIDTH + 2 * KV_WIDTH
    gu = _gelu(proj[:, g0:g0 + GMLP_WIDTH])
    gv = _gelu(proj[:, g0 + GMLP_WIDTH:g0 + 2 * GMLP_WIDTH])

    def head_mean(t):
        hi, lo = _split_bf16(t)
        return _dot(hi, amat_ref[...]) + _dot(lo, amat_ref[...])

    mu = head_mean(gv)
    gc = gv - mu
    var_h = head_mean(gc * gc)
    vn = gc * lax.rsqrt(var_h + LN_EPS) * gng_ref[...] + gnb_ref[...]
    for ci in range(nblk):
        r0 = ci * blk
        for j in range(GMLP_WIDTH // LANES):
            vp = vn[r0:r0 + blk, j * LANES:(j + 1) * LANES]
            rhs = jnp.concatenate([jnp.where(lo_half, vp, 0.0).astype(BF16),
                                   jnp.where(lo_half, 0.0, vp).astype(BF16)], axis=0)
            mixed = _dot(wsp_s[j], rhs)
            gm = gu[r0:r0 + blk, j * LANES:(j + 1) * LANES] * (
                mixed + bsf_ref[:, j * LANES:(j + 1) * LANES])
            ocat_s[r0:r0 + blk, ATTN_WIDTH + j * LANES:ATTN_WIDTH + (j + 1) * LANES] = (
                gm.astype(BF16))

    mix = _dot(ocat_s[...], wout_ref[...]) + bout_ref[...]
    h1 = _layer_norm(DEEPNORM_ALPHA * h0 + mix, ln1g_ref[...], ln1b_ref[...])
    h1_ref[...] = h1

    hi, lo = _split_bf16(h1)
    logits = (_dot(hi, wrh_ref[...]) + _dot(lo, wrh_ref[...]) + _dot(hi, wrl_ref[...])
              + br_ref[...])
    eidx = lax.broadcasted_iota(jnp.int32, logits.shape, 1)
    vals = []
    cur = logits
    for k in range(TOP_K):
        m = jnp.max(cur, axis=-1, keepdims=True)
        sel = jnp.min(jnp.where(cur == m, eidx, N_EXPERTS), axis=-1, keepdims=True)
        vals.append(m)
        idx_ref[:, k:k + 1] = sel
        cur = jnp.where(eidx == sel, _NEG_INF, cur)
    exps = [jnp.exp(v - vals[0]) for v in vals]
    den = exps[0] + exps[1] + exps[2] + exps[3]
    for k in range(TOP_K):
        gate_ref[:, k:k + 1] = exps[k] / den


def _mixer(sinks, x2, pos2, lng, lnb, win, bin_, frow, gng, gnb, ws, bsf, amat, wout, bout,
           ln1g, ln1b, wrh, wrl, br):
    t = x2.shape[0]
    rows = MIX_ROWS

    def const(shape):
        nd = len(shape)
        return pl.BlockSpec(shape, lambda i, _nd=nd: (0,) * _nd)

    in_specs = [
        pl.BlockSpec(memory_space=pltpu.SMEM),
        pl.BlockSpec((rows, D_MODEL), lambda i: (i, 0)),
        pl.BlockSpec((rows, 1), lambda i: (i, 0)),
        const((1, D_MODEL)), const((1, D_MODEL)),
        const((D_MODEL, IN_WIDTH)), const((1, IN_WIDTH)),
        const((1, LANES)),
        const((1, GMLP_WIDTH)), const((1, GMLP_WIDTH)),
        const((N_G_HEADS, ATTN_BLOCK, ATTN_BLOCK)),
        const((ATTN_BLOCK, GMLP_WIDTH)),
        const((GMLP_WIDTH, GMLP_WIDTH)),
        const((D_MODEL, D_MODEL)), const((1, D_MODEL)),
        const((1, D_MODEL)), const((1, D_MODEL)),
        const((D_MODEL, N_EXPERTS)), const((D_MODEL, N_EXPERTS)),
        const((1, N_EXPERTS)),
    ]
    out_specs = [
        pl.BlockSpec((rows, D_MODEL), lambda i: (i, 0)),
        pl.BlockSpec((rows, TOP_K), lambda i: (i, 0)),
        pl.BlockSpec((rows, TOP_K), lambda i: (i, 0)),
    ]
    out_shape = [
        jax.ShapeDtypeStruct((t, D_MODEL), F32),
        jax.ShapeDtypeStruct((t, TOP_K), jnp.int32),
        jax.ShapeDtypeStruct((t, TOP_K), F32),
    ]
    scratch = [
        pltpu.VMEM((N_G_HEADS // 2, ATTN_BLOCK, 2 * ATTN_BLOCK), BF16),
        pltpu.VMEM((4, 2 * ATTN_BLOCK, LANES), BF16),
        pltpu.VMEM((4, 2 * ATTN_BLOCK, LANES), BF16),
        pltpu.VMEM((rows, D_MODEL), BF16),
    ]
    return pl.pallas_call(
        _mix_kernel,
        grid=(t // rows,),
        in_specs=in_specs,
        out_specs=out_specs,
        out_shape=out_shape,
        scratch_shapes=scratch,
        compiler_params=pltpu.CompilerParams(
            dimension_semantics=("arbitrary",), vmem_limit_bytes=VMEM_LIMIT_BYTES),
        name="mixer",
    )(sinks, x2, pos2, lng, lnb, win, bin_, frow, gng, gnb, ws, bsf, amat, wout, bout,
      ln1g, ln1b, wrh, wrl, br)


def _moe_kernel(blk_e_ref, nu_ref, src0_ref, src1_ref, dst_ref, h1_hbm, wgu_ref, bgu_ref,
                wd_ref, bd_ref, y_hbm, xbuf, ybuf, wgu_s, wd_s, gsem, ssem):
    b = pl.program_id(0)
    nu = nu_ref[0]
    slot = lax.rem(b, 2)
    nslot = 1 - slot
    rows = EXPERT_ROWS

    def gather(src_ref, sl):
        def body(r, c):
            tok = src_ref[0, 0, r]
            pltpu.make_async_copy(h1_hbm.at[pl.ds(tok, 1)], xbuf.at[sl, pl.ds(r, 1)],
                                  gsem.at[sl]).start()
            return c
        lax.fori_loop(0, rows, body, 0, unroll=8)

    def wait_gather(sl):
        pltpu.make_async_copy(h1_hbm.at[pl.ds(0, rows)], xbuf.at[sl], gsem.at[sl]).wait()

    def wait_scatter(sl):
        pltpu.make_async_copy(ybuf.at[sl], y_hbm.at[pl.ds(0, rows)], ssem.at[sl]).wait()

    @pl.when(b == 0)
    def _first_gather():
        gather(src0_ref, 0)
        spare0 = y_hbm.shape[0] - N_EXPERTS * rows
        ybuf[1] = jnp.zeros((rows, D_MODEL), F32)
        for e in range(N_EXPERTS):
            pltpu.make_async_copy(ybuf.at[1], y_hbm.at[pl.ds(spare0 + e * rows, rows)],
                                  ssem.at[1]).start()
        for e in range(N_EXPERTS):
            wait_scatter(1)

    @pl.when(b + 1 < nu)
    def _next_gather():
        gather(src1_ref, nslot)

    changed = (b == 0) | (blk_e_ref[b] != blk_e_ref[jnp.maximum(b - 1, 0)])

    @pl.when(changed & (b < nu))
    def _cast_weights():
        wgu_s[...] = wgu_ref[0].astype(BF16)
        wd_s[...] = wd_ref[0].astype(BF16)

    @pl.when(b < nu)
    def _block():
        wait_gather(slot)

        @pl.when(b >= 2)
        def _reuse():
            wait_scatter(slot)

        x = xbuf[slot].astype(BF16)
        acc = None
        for c in range(D_EXPERT // EXPERT_CHUNK):
            c0 = c * EXPERT_CHUNK
            gate = _dot(x, wgu_s[:, c0:c0 + EXPERT_CHUNK]) + bgu_ref[0, :, c0:c0 + EXPERT_CHUNK]
            up = (_dot(x, wgu_s[:, D_EXPERT + c0:D_EXPERT + c0 + EXPERT_CHUNK])
                  + bgu_ref[0, :, D_EXPERT + c0:D_EXPERT + c0 + EXPERT_CHUNK])
            gate = jnp.minimum(gate, SWIGLU_LIMIT)
            up = jnp.clip(up, -SWIGLU_LIMIT, SWIGLU_LIMIT)
            sig = 1.0 / (1.0 + jnp.exp(-(gate * SWIGLU_ALPHA)))
            hmid = ((up + 1.0) * (gate * sig)).astype(BF16)
            d = _dot(hmid, wd_s[c0:c0 + EXPERT_CHUNK, :])
            acc = d if acc is None else acc + d
        ybuf[slot] = acc + bd_ref[0]

        def sbody(r, c):
            row = dst_ref[0, 0, r]
            pltpu.make_async_copy(ybuf.at[slot, pl.ds(r, 1)], y_hbm.at[pl.ds(row, 1)],
                                  ssem.at[slot]).start()
            return c
        lax.fori_loop(0, rows, sbody, 0, unroll=8)

        @pl.when(b == nu - 1)
        def _drain():
            @pl.when(b >= 1)
            def _prev():
                wait_scatter(nslot)
            wait_scatter(slot)


def _experts(blk_e, nu, src, dst, h1, wgu, bgu, wd, bd, n_rows_out):
    nblk = blk_e.shape[0]
    rows = EXPERT_ROWS
    smem = pltpu.SMEM
    grid_spec = pltpu.PrefetchScalarGridSpec(
        num_scalar_prefetch=2,
        grid=(nblk,),
        in_specs=[
            pl.BlockSpec((1, 1, rows), lambda b, e, n: (0, 0, 0), memory_space=smem),
            pl.BlockSpec((1, 1, rows), lambda b, e, n: (jnp.minimum(b + 1, nblk - 1), 0, 0),
                         memory_space=smem),
            pl.BlockSpec((1, 1, rows), lambda b, e, n: (b, 0, 0), memory_space=smem),
            pl.BlockSpec(memory_space=pl.ANY),
            pl.BlockSpec((1, D_MODEL, 2 * D_EXPERT), lambda b, e, n: (e[b], 0, 0)),
            pl.BlockSpec((1, 1, 2 * D_EXPERT), lambda b, e, n: (e[b], 0, 0)),
            pl.BlockSpec((1, D_EXPERT, D_MODEL), lambda b, e, n: (e[b], 0, 0)),
            pl.BlockSpec((1, 1, D_MODEL), lambda b, e, n: (e[b], 0, 0)),
        ],
        out_specs=pl.BlockSpec(memory_space=pl.ANY),
        scratch_shapes=[
            pltpu.VMEM((2, rows, D_MODEL), F32),
            pltpu.VMEM((2, rows, D_MODEL), F32),
            pltpu.VMEM((D_MODEL, 2 * D_EXPERT), BF16),
            pltpu.VMEM((D_EXPERT, D_MODEL), BF16),
            pltpu.SemaphoreType.DMA((2,)),
            pltpu.SemaphoreType.DMA((2,)),
        ],
    )
    return pl.pallas_call(
        _moe_kernel,
        grid_spec=grid_spec,
        out_shape=jax.ShapeDtypeStruct((n_rows_out, D_MODEL), F32),
        compiler_params=pltpu.CompilerParams(
            dimension_semantics=("arbitrary",), vmem_limit_bytes=VMEM_LIMIT_BYTES),
        name="experts",
    )(blk_e, nu, src, src, dst, h1, wgu, bgu, wd, bd)


def _fin_kernel(y_ref, g_ref, h1_ref, lg_ref, lb_ref, o_ref):
    g = g_ref[...]
    ffn = y_ref[:, 0:D_MODEL] * g[:, 0:1]
    for k in range(1, TOP_K):
        ffn = ffn + y_ref[:, k * D_MODEL:(k + 1) * D_MODEL] * g[:, k:k + 1]
    o_ref[...] = _layer_norm(DEEPNORM_ALPHA * h1_ref[...] + ffn, lg_ref[...], lb_ref[...])


def _combine(y4, gates, h1, lg, lb):
    t = h1.shape[0]
    rows = FIN_ROWS
    return pl.pallas_call(
        _fin_kernel,
        grid=(t // rows,),
        in_specs=[
            pl.BlockSpec((rows, TOP_K * D_MODEL), lambda i: (i, 0)),
            pl.BlockSpec((rows, TOP_K), lambda i: (i, 0)),
            pl.BlockSpec((rows, D_MODEL), lambda i: (i, 0)),
            pl.BlockSpec((1, D_MODEL), lambda i: (0, 0)),
            pl.BlockSpec((1, D_MODEL), lambda i: (0, 0)),
        ],
        out_specs=pl.BlockSpec((rows, D_MODEL), lambda i: (i, 0)),
        out_shape=jax.ShapeDtypeStruct((t, D_MODEL), F32),
        compiler_params=pltpu.CompilerParams(
            dimension_semantics=("arbitrary",), vmem_limit_bytes=VMEM_LIMIT_BYTES),
        name="combine",
    )(y4, gates, h1, lg, lb)


def _routing_tables(top_idx):
    rows = EXPERT_ROWS
    n = top_idx.shape[0] * TOP_K
    n_pad = n + N_EXPERTS * rows
    nblk = n_pad // rows
    flat_e = top_idx.reshape(n)
    sorted_e, order = lax.sort((flat_e, lax.iota(jnp.int32, n)), num_keys=1, is_stable=True)
    starts = jnp.searchsorted(sorted_e, jnp.arange(N_EXPERTS + 1, dtype=jnp.int32),
                              side="left").astype(jnp.int32)
    counts = starts[1:] - starts[:-1]
    starts = starts[:-1]
    padded = ((counts + rows - 1) // rows) * rows
    pends = jnp.cumsum(padded)
    pstarts = pends - padded
    blk_start = jnp.arange(nblk, dtype=jnp.int32) * rows
    blk_e = jnp.minimum(jnp.searchsorted(pends, blk_start, side="right"),
                        N_EXPERTS - 1).astype(jnp.int32)
    n_used = (pends[-1] // rows).astype(jnp.int32).reshape(1)
    within = jnp.arange(rows, dtype=jnp.int32)[None, :]
    j = (blk_start - pstarts[blk_e])[:, None] + within
    valid = (j < counts[blk_e][:, None]) & (blk_start[:, None] < pends[-1])
    copy = order[jnp.clip(starts[blk_e][:, None] + j, 0, n - 1)]
    src = jnp.where(valid, copy // TOP_K, 0).astype(jnp.int32)
    trash = n + blk_e[:, None] * rows + within
    dst = jnp.where(valid, copy, trash).astype(jnp.int32)
    return blk_e, n_used, src.reshape(nblk, 1, rows), dst.reshape(nblk, 1, rows), n_pad


def kernel(x, positions, ln_in_g, ln_in_b, w_in, b_in, sinks, gn_g, gn_b, w_s, b_s, w_out, b_out,
           ln1_g, ln1_b, w_router, b_router, w_gate_up, b_gate_up, w_down, b_down, ln2_g, ln2_b):
    assert w_in.shape[0] == 1, "single-layer block"
    bsz, seq, d = x.shape
    assert (seq, d) == (SEQ, D_MODEL)
    t = bsz * seq
    x2 = x.reshape(t, d)
    pos2 = positions.reshape(t, 1)

    lane = np.arange(LANES) % HEAD_DIM
    inv_freq = ROPE_THETA ** (-jnp.arange(0, ROPE_DIM, 2, dtype=F32) / ROPE_DIM)
    frow = jnp.where(lane < ROPE_DIM, inv_freq[lane % (ROPE_DIM // 2)], 0.0).reshape(1, LANES)
    amat = jnp.asarray(np.kron(np.eye(N_G_HEADS), np.full((HEAD_DIM, HEAD_DIM), 1.0 / HEAD_DIM)),
                       dtype=BF16)
    wr = w_router[0]
    wrh = wr.astype(BF16)
    wrl = (wr - wrh.astype(F32)).astype(BF16)

    h1, top_idx, gates = _mixer(
        sinks[0], x2, pos2, ln_in_g.reshape(1, d), ln_in_b.reshape(1, d),
        w_in[0].astype(BF16), b_in[0].reshape(1, IN_WIDTH), frow,
        jnp.tile(gn_g[0], N_G_HEADS).reshape(1, GMLP_WIDTH),
        jnp.tile(gn_b[0], N_G_HEADS).reshape(1, GMLP_WIDTH),
        w_s[0], jnp.repeat(b_s[0].T, HEAD_DIM, axis=1), amat,
        w_out[0].astype(BF16), b_out[0].reshape(1, d),
        ln1_g[0].reshape(1, d), ln1_b[0].reshape(1, d),
        wrh, wrl, b_router[0].reshape(1, N_EXPERTS))

    blk_e, n_used, src, dst, n_pad = _routing_tables(top_idx)
    y = _experts(blk_e, n_used, src, dst, h1, w_gate_up[0],
                 b_gate_up[0].reshape(N_EXPERTS, 1, 2 * D_EXPERT), w_down[0],
                 b_down[0].reshape(N_EXPERTS, 1, d), n_pad)
    y4 = y.reshape(n_pad // TOP_K, TOP_K * d)
    out = _combine(y4, gates, h1, ln2_g[0].reshape(1, d), ln2_b[0].reshape(1, d))
    return out.reshape(bsz, seq, d)
```
